```python
import functools
import jax, jax.numpy as jnp
from jax import lax
import numpy as np

D_MODEL = 4096
BATCH = 1
SEQ = 16384
DEPTH = 2

CTX_LEN = 256
GRID_W = 64
Q_BLOCK = 128
ROPE_THETA = 10000.0
NORM_EPS = 1e-6
N_BRANCH = 3
BRANCH_WIDTH = D_MODEL // 2
MLA_HEADS = 16
MLA_Q_RANK = 768
MLA_KV_RANK = 512
MLA_NOPE_DIM = 128
MLA_ROPE_DIM = 64
MLA_V_DIM = BRANCH_WIDTH // MLA_HEADS
MLA_QK_DIM = MLA_NOPE_DIM + MLA_ROPE_DIM
MLA_SCALE = MLA_QK_DIM ** -0.5
GQA_HEAD_DIM = 128
GQA_HEADS = BRANCH_WIDTH // GQA_HEAD_DIM
GQA_KV_HEADS = 4
GQA_REP = GQA_HEADS // GQA_KV_HEADS
GQA_SCALE = GQA_HEAD_DIM ** -0.5
CONV_CH = BRANCH_WIDTH
CONV_K = 3
FFN_DIM = 11008
N_EXPERTS = 8
TOP_K = 2
EXPERT_DIM = FFN_DIM // 4
KV_COLS = MLA_KV_RANK + MLA_ROPE_DIM + 2 * GQA_KV_HEADS * GQA_HEAD_DIM
Q_COLS = MLA_Q_RANK + GQA_HEADS * GQA_HEAD_DIM + 3 * CONV_CH + N_BRANCH * D_MODEL
IN_COLS = KV_COLS + Q_COLS

kernel_name = "hybrid_mla_gqa_shortconv_moe_dit"


def rmsnorm(x, g):
    x32 = x.astype(jnp.float32)
    y = x32 * lax.rsqrt(jnp.mean(x32 * x32, axis=-1, keepdims=True) + NORM_EPS)
    return y.astype(x.dtype) * g


def axial_tables(n_tok, dim):
    rows = n_tok // GRID_W
    row_ids = jnp.repeat(jnp.arange(rows), GRID_W).astype(jnp.float32)
    col_ids = (jnp.arange(rows * GRID_W) % GRID_W).astype(jnp.float32)
    half = dim // 2
    inv = ROPE_THETA ** (-jnp.arange(0, half, 2, dtype=jnp.float32) / half)
    ang_r = row_ids[:, None] * inv[None, :]
    ang_c = col_ids[:, None] * inv[None, :]
    return (jnp.cos(ang_r), jnp.sin(ang_r), jnp.cos(ang_c), jnp.sin(ang_c))


def _rotate(x, cos, sin):
    x1, x2 = jnp.split(x, 2, axis=-1)
    return jnp.concatenate([x1 * cos - x2 * sin, x2 * cos + x1 * sin], axis=-1)


def axial_rope(x, tables):
    tabs = [t.astype(x.dtype) for t in tables]
    if x.ndim == 4:
        tabs = [t[:, None, :] for t in tabs]
    cr, sr, cc, sc = tabs
    xr, xc = jnp.split(x, 2, axis=-1)
    return jnp.concatenate([_rotate(xr, cr, sr), _rotate(xc, cc, sc)], axis=-1)


def _to_blocks(a):
    b, t = a.shape[:2]
    return jnp.moveaxis(a.reshape(b, t // Q_BLOCK, Q_BLOCK, *a.shape[2:]), 1, 0)


def _from_blocks(a):
    a = jnp.moveaxis(a, 0, 1)
    return a.reshape(a.shape[0], a.shape[1] * a.shape[2], *a.shape[3:])


def sweep_query_blocks(block_fn, queries, keys):
    out = lax.map(lambda qb: block_fn(*qb, *keys), tuple(_to_blocks(q) for q in queries))
    return _from_blocks(out)


def mla_block(q_nope, q_rope, k_nope, k_rope, v):
    s = (jnp.einsum("bqhn,bkhn->bhqk", q_nope, k_nope)
         + jnp.einsum("bqhr,bkr->bhqk", q_rope, k_rope))
    p = jax.nn.softmax(s.astype(jnp.float32) * MLA_SCALE, axis=-1).astype(v.dtype)
    return jnp.einsum("bhqk,bkhv->bqhv", p, v)


def gqa_block(q, k, v):
    s = jnp.einsum("bqgrd,bkgd->bgrqk", q, k)
    p = jax.nn.softmax(s.astype(jnp.float32) * GQA_SCALE, axis=-1).astype(v.dtype)
    return jnp.einsum("bgrqk,bkgd->bqgrd", p, v)


def short_conv(u, w, b):
    t = u.shape[1]
    pad = CONV_K // 2
    up = jnp.pad(u, ((0, 0), (pad, pad), (0, 0)))
    out = up[:, 0:t] * w[0]
    for k in range(1, CONV_K):
        out = out + up[:, k:k + t] * w[k]
    return out + b


def ada_mod(cvec, w, b):
    mod = jax.nn.silu(cvec) @ w + b
    return [m[:, None, :] for m in jnp.split(mod, 6, axis=-1)]


def modulate(x, shift, scale):
    return x * (1 + scale) + shift


def kv_side(z, lp, tabs):
    b, t = z.shape[:2]
    o1 = MLA_KV_RANK
    o2 = o1 + MLA_ROPE_DIM
    o3 = o2 + GQA_KV_HEADS * GQA_HEAD_DIM
    c_kv, k_rope, gk, gv = jnp.split(z, [o1, o2, o3], axis=-1)
    kv = (rmsnorm(c_kv, lp["mla_kv_a_norm"]) @ lp["mla_w_ukv"]).reshape(b, t, MLA_HEADS, MLA_NOPE_DIM + MLA_V_DIM)
    k_nope = rmsnorm(kv[..., :MLA_NOPE_DIM], lp["mla_kn_norm"])
    v = kv[..., MLA_NOPE_DIM:]
    k_rope = rmsnorm(k_rope, lp["mla_kr_norm"])
    gk = rmsnorm(gk.reshape(b, t, GQA_KV_HEADS, GQA_HEAD_DIM), lp["gqa_k_norm"])
    gv = gv.reshape(b, t, GQA_KV_HEADS, GQA_HEAD_DIM)
    if tabs is not None:
        k_rope = axial_rope(k_rope, tabs[0])
        gk = axial_rope(gk, tabs[1])
    return (k_nope, k_rope, v, gk, gv)


def mix_queries(z, keys, lp, tabs):
    b, t = z.shape[:2]
    offs = np.cumsum([MLA_Q_RANK, GQA_HEADS * GQA_HEAD_DIM, CONV_CH, CONV_CH, CONV_CH]).tolist()
    c_q, gq, u_in, gate_b, gate_c, branch_gates = jnp.split(z, offs, axis=-1)
    qa = (rmsnorm(c_q, lp["mla_q_a_norm"]) @ lp["mla_w_uq"]).reshape(b, t, MLA_HEADS, MLA_QK_DIM)
    q_nope = rmsnorm(qa[..., :MLA_NOPE_DIM], lp["mla_qn_norm"])
    q_rope = rmsnorm(qa[..., MLA_NOPE_DIM:], lp["mla_qr_norm"])
    q = rmsnorm(gq.reshape(b, t, GQA_HEADS, GQA_HEAD_DIM), lp["gqa_q_norm"])
    if tabs is not None:
        q_rope = axial_rope(q_rope, tabs[0])
        q = axial_rope(q, tabs[1])
    q = q.reshape(b, t, GQA_KV_HEADS, GQA_REP, GQA_HEAD_DIM)
    k_nope, k_rope, v, gk, gv = keys
    y_mla = sweep_query_blocks(mla_block, (q_nope, q_rope), (k_nope, k_rope, v)).reshape(b, t, MLA_HEADS * MLA_V_DIM)
    y_gqa = sweep_query_blocks(gqa_block, (q,), (gk, gv)).reshape(b, t, GQA_HEADS * GQA_HEAD_DIM)
    y_conv = gate_b * short_conv(gate_c * u_in, lp["conv_w"], lp["conv_b"])
    g_mla, g_gqa, g_conv = jnp.split(jax.nn.sigmoid(branch_gates), N_BRANCH, axis=-1)
    merged = (g_mla * (y_mla @ lp["w_br_mla"])
              + g_gqa * (y_gqa @ lp["w_br_gqa"])
              + g_conv * (y_conv @ lp["w_br_conv"]))
    return merged @ lp["w_out"]


def swiglu(t, w_gate, w_up, w_down):
    return (jax.nn.silu(t @ w_gate) * (t @ w_up)) @ w_down


def moe_swiglu(t, router, router_b, w_gate, w_up, w_down):
    logits = (t @ router).astype(jnp.float32) + router_b.astype(jnp.float32)
    top_v, top_i = lax.top_k(logits, TOP_K)
    top_w = jax.nn.softmax(top_v, axis=-1)
    comb = jnp.sum(jax.nn.one_hot(top_i, N_EXPERTS, dtype=jnp.float32) * top_w[..., None], axis=-2).astype(t.dtype)
    out = jnp.zeros_like(t)
    for e in range(N_EXPERTS):
        out = out + comb[..., e:e + 1] * swiglu(t, w_gate[e], w_up[e], w_down[e])
    return out


def setup_inputs(seed: int = 0) -> dict:
    key = jax.random.key(seed)
    ks = iter(jax.random.split(key, 48))
    L = DEPTH
    nd = (DEPTH + 1) // 2
    nm = DEPTH // 2

    def nrm(shape, scale):
        return jax.random.normal(next(ks), shape, jnp.float32) * scale

    def gain(shape):
        return 1.0 + nrm(shape, 0.02)

    D = D_MODEL
    return {
        "x": nrm((BATCH, SEQ, D), 1.0),
        "c": nrm((BATCH, D), 1.0),
        "ctx": nrm((BATCH, CTX_LEN, D), 1.0),
        "c_ctx": nrm((D,), 1.0),
        "ada_w": nrm((L, D, 6 * D), 0.5 * D ** -0.5),
        "ada_b": nrm((L, 6 * D), 0.01),
        "norm_mix": gain((L, D)),
        "norm_ffn": gain((L, D)),
        "w_in": nrm((L, D, IN_COLS), D ** -0.5),
        "mla_q_a_norm": gain((L, MLA_Q_RANK)),
        "mla_kv_a_norm": gain((L, MLA_KV_RANK)),
        "mla_w_uq": nrm((L, MLA_Q_RANK, MLA_HEADS * MLA_QK_DIM), MLA_Q_RANK ** -0.5),
        "mla_w_ukv": nrm((L, MLA_KV_RANK, MLA_HEADS * (MLA_NOPE_DIM + MLA_V_DIM)), MLA_KV_RANK ** -0.5),
        "mla_qn_norm": gain((L, MLA_NOPE_DIM)),
        "mla_qr_norm": gain((L, MLA_ROPE_DIM)),
        "mla_kn_norm": gain((L, MLA_NOPE_DIM)),
        "mla_kr_norm": gain((L, MLA_ROPE_DIM)),
        "gqa_q_norm": gain((L, GQA_HEAD_DIM)),
        "gqa_k_norm": gain((L, GQA_HEAD_DIM)),
        "conv_w": nrm((L, CONV_K, CONV_CH), CONV_K ** -0.5),
        "conv_b": nrm((L, CONV_CH), 0.01),
        "w_br_mla": nrm((L, MLA_HEADS * MLA_V_DIM, D), (MLA_HEADS * MLA_V_DIM) ** -0.5),
        "w_br_gqa": nrm((L, GQA_HEADS * GQA_HEAD_DIM, D), (GQA_HEADS * GQA_HEAD_DIM) ** -0.5),
        "w_br_conv": nrm((L, CONV_CH, D), CONV_CH ** -0.5),
        "w_out": nrm((L, D, D), D ** -0.5),
        "ffn_w_gate": nrm((nd, D, FFN_DIM), D ** -0.5),
        "ffn_w_up": nrm((nd, D, FFN_DIM), D ** -0.5),
        "ffn_w_down": nrm((nd, FFN_DIM, D), FFN_DIM ** -0.5),
        "moe_router": nrm((nm, D, N_EXPERTS), D ** -0.5),
        "moe_router_b": nrm((nm, N_EXPERTS), 0.01),
        "moe_w_gate": nrm((nm, N_EXPERTS, D, EXPERT_DIM), D ** -0.5),
        "moe_w_up": nrm((nm, N_EXPERTS, D, EXPERT_DIM), D ** -0.5),
        "moe_w_down": nrm((nm, N_EXPERTS, EXPERT_DIM, D), EXPERT_DIM ** -0.5),
    }


def reference(x, c, ctx, c_ctx, ada_w, ada_b, norm_mix, norm_ffn, w_in,
              mla_q_a_norm, mla_kv_a_norm, mla_w_uq, mla_w_ukv,
              mla_qn_norm, mla_qr_norm, mla_kn_norm, mla_kr_norm,
              gqa_q_norm, gqa_k_norm, conv_w, conv_b,
              w_br_mla, w_br_gqa, w_br_conv, w_out,
              ffn_w_gate, ffn_w_up, ffn_w_down,
              moe_router, moe_router_b, moe_w_gate, moe_w_up, moe_w_down):
    n_tok = x.shape[1]
    tabs = (axial_tables(n_tok, MLA_ROPE_DIM), axial_tables(n_tok, GQA_HEAD_DIM))
    h, hc = x, ctx
    for i in range(DEPTH):
        last = i == DEPTH - 1
        lp = {
            "mla_q_a_norm": mla_q_a_norm[i], "mla_kv_a_norm": mla_kv_a_norm[i],
            "mla_w_uq": mla_w_uq[i], "mla_w_ukv": mla_w_ukv[i],
            "mla_qn_norm": mla_qn_norm[i], "mla_qr_norm": mla_qr_norm[i],
            "mla_kn_norm": mla_kn_norm[i], "mla_kr_norm": mla_kr_norm[i],
            "gqa_q_norm": gqa_q_norm[i], "gqa_k_norm": gqa_k_norm[i],
            "conv_w": conv_w[i], "conv_b": conv_b[i],
            "w_br_mla": w_br_mla[i], "w_br_gqa": w_br_gqa[i], "w_br_conv": w_br_conv[i],
            "w_out": w_out[i],
        }
        ml = ada_mod(c, ada_w[i], ada_b[i])
        mc = ada_mod(c_ctx[None, :], ada_w[i], ada_b[i])
        xl = modulate(rmsnorm(h, norm_mix[i]), ml[0], ml[1])
        xc = modulate(rmsnorm(hc, norm_mix[i]), mc[0], mc[1])
        w_in_i = w_in[i]
        zl = xl @ w_in_i
        zc = xc @ (w_in_i[:, :KV_COLS] if last else w_in_i)
        kv_l = kv_side(zl[..., :KV_COLS], lp, tabs)
        kv_c = kv_side(zc[..., :KV_COLS], lp, None)
        keys_l = tuple(jnp.concatenate([a_c, a_l], axis=1) for a_c, a_l in zip(kv_c, kv_l))
        h_new = h + ml[2] * mix_queries(zl[..., KV_COLS:], keys_l, lp, tabs)
        if not last:
            hc = hc + mc[2] * mix_queries(zc[..., KV_COLS:], kv_c, lp, None)
        h = h_new
        j = i // 2
        if i % 2 == 0:
            ffn = functools.partial(swiglu, w_gate=ffn_w_gate[j], w_up=ffn_w_up[j], w_down=ffn_w_down[j])
        else:
            ffn = functools.partial(moe_swiglu, router=moe_router[j], router_b=moe_router_b[j],
                                    w_gate=moe_w_gate[j], w_up=moe_w_up[j], w_down=moe_w_down[j])
        h = h + ml[5] * ffn(modulate(rmsnorm(h, norm_ffn[i]), ml[3], ml[4]))
        if not last:
            hc = hc + mc[5] * ffn(modulate(rmsnorm(hc, norm_ffn[i]), mc[3], mc[4]))
    return h
```

```python
import functools

import jax
import jax.numpy as jnp
from jax import lax
from jax.experimental import pallas as pl
from jax.experimental.pallas import tpu as pltpu

F32 = jnp.float32
BF16 = jnp.bfloat16

GRID_W = 64
ROPE_THETA = 10000.0
NORM_EPS = 1e-6
MLA_NOPE_DIM = 128
MLA_ROPE_DIM = 64
MLA_V_DIM = 128
GQA_HEAD_DIM = 128
N_BRANCH = 3
CONV_K = 3

LANE = 128
BF16_SUBLANES = 16
VMEM_LIMIT_BYTES = 56 * 1024 * 1024

ROW_TILE = 256
ATTN_KV_CHUNK = 1024


def _cparams(*semantics):
    return pltpu.CompilerParams(dimension_semantics=semantics, vmem_limit_bytes=VMEM_LIMIT_BYTES)


def _pick(n, prefs):
    for p in prefs:
        if p <= n and n % p == 0:
            return p
    return n


def _round_up(n, m):
    return -(-n // m) * m


def _ada_kernel(c_ref, w_ref, b_ref, o_ref):
    c = c_ref[...]
    s = c * jax.nn.sigmoid(c)
    o_ref[0] = jnp.dot(s, w_ref[0], precision=lax.Precision.HIGHEST,
                       preferred_element_type=F32) + b_ref[0]


def _ada_mod(cvecs, ada_w, ada_b):
    n_layers, d, n = ada_w.shape
    tn = _pick(n, (512, 256, 128))
    return pl.pallas_call(
        _ada_kernel,
        grid=(n_layers, n // tn),
        in_specs=[
            pl.BlockSpec((8, d), lambda l, j: (0, 0)),
            pl.BlockSpec((1, d, tn), lambda l, j: (l, 0, j)),
            pl.BlockSpec((1, 1, tn), lambda l, j: (l, 0, j)),
        ],
        out_specs=pl.BlockSpec((1, 8, tn), lambda l, j: (l, 0, j)),
        out_shape=jax.ShapeDtypeStruct((n_layers, 8, n), F32),
        compiler_params=_cparams("parallel", "parallel"),
        name="ada_mod",
    )(cvecs, ada_w, ada_b.reshape(n_layers, 1, n))


def _norm_mod_kernel(x_ref, g_ref, sh_ref, sc_ref, o_ref):
    x = x_ref[...]
    r = lax.rsqrt(jnp.mean(x * x, axis=-1, keepdims=True) + NORM_EPS)
    a = g_ref[...] * (1.0 + sc_ref[0])
    o_ref[...] = ((x * r) * a + sh_ref[0]).astype(o_ref.dtype)


def _mod_index(n_lat_tiles, which):
    return lambda i, *_: (jnp.where(i >= n_lat_tiles, 6, 0) + which, 0, 0)


def _norm_mod(h, gain, mods, which_shift, n_lat, rows):
    d = h.shape[1]
    tm = ROW_TILE
    nl = n_lat // tm
    return pl.pallas_call(
        _norm_mod_kernel,
        grid=(rows // tm,),
        in_specs=[
            pl.BlockSpec((tm, d), lambda i: (i, 0)),
            pl.BlockSpec((1, d), lambda i: (0, 0)),
            pl.BlockSpec((1, 1, d), _mod_index(nl, which_shift)),
            pl.BlockSpec((1, 1, d), _mod_index(nl, which_shift + 1)),
        ],
        out_specs=pl.BlockSpec((tm, d), lambda i: (i, 0)),
        out_shape=jax.ShapeDtypeStruct((h.shape[0], d), BF16),
        compiler_params=_cparams("parallel"),
        name="norm_mod",
    )(h, gain.reshape(1, d), mods, mods)


def _mm_kernel(x_ref, w_ref, o_ref):
    o_ref[...] = jnp.dot(x_ref[...], w_ref[...], preferred_element_type=F32).astype(o_ref.dtype)


def _matmul(x, w, rows, tm, tn, out_dtype, name):
    k = x.shape[1]
    n = w.shape[1]
    return pl.pallas_call(
        _mm_kernel,
        grid=(rows // tm, n // tn),
        in_specs=[
            pl.BlockSpec((tm, k), lambda i, j: (i, 0)),
            pl.BlockSpec((k, tn), lambda i, j: (0, j)),
        ],
        out_specs=pl.BlockSpec((tm, tn), lambda i, j: (i, j)),
        out_shape=jax.ShapeDtypeStruct((x.shape[0], n), out_dtype),
        compiler_params=_cparams("parallel", "parallel"),
        name=name,
    )(x, w)


def _mm_res_kernel(a_ref, w_ref, res_ref, gate_lat_ref, gate_ctx_ref, o_ref, acc_ref, *, nk, tm, n_lat):
    k = pl.program_id(2)
    part = jnp.dot(a_ref[...], w_ref[...], preferred_element_type=F32)

    def finish(total):
        row = pl.program_id(0) * tm + lax.broadcasted_iota(jnp.int32, (tm, 1), 0)
        gate = jnp.where(row < n_lat, gate_lat_ref[0], gate_ctx_ref[0])
        o_ref[...] = res_ref[...] + gate * total

    if nk == 1:
        finish(part)
        return

    @pl.when(k == 0)
    def _():
        acc_ref[...] = part

    @pl.when(jnp.logical_and(k > 0, k < nk - 1))
    def _():
        acc_ref[...] += part

    @pl.when(k == nk - 1)
    def _():
        finish(acc_ref[...] + part)


def _matmul_residual(a, w, res, mods, which_gate, n_lat, rows, tm, tn, tk, name):
    kdim = a.shape[1]
    n = w.shape[1]
    nk = kdim // tk
    acc_shape = (tm, tn) if nk > 1 else (8, LANE)
    return pl.pallas_call(
        functools.partial(_mm_res_kernel, nk=nk, tm=tm, n_lat=n_lat),
        grid=(rows // tm, n // tn, nk),
        in_specs=[
            pl.BlockSpec((tm, tk), lambda i, j, k: (i, k)),
            pl.BlockSpec((tk, tn), lambda i, j, k: (k, j)),
            pl.BlockSpec((tm, tn), lambda i, j, k: (i, j)),
            pl.BlockSpec((1, 1, tn), lambda i, j, k: (which_gate, 0, j)),
            pl.BlockSpec((1, 1, tn), lambda i, j, k: (6 + which_gate, 0, j)),
        ],
        out_specs=pl.BlockSpec((tm, tn), lambda i, j, k: (i, j)),
        out_shape=jax.ShapeDtypeStruct(res.shape, F32),
        scratch_shapes=[pltpu.VMEM(acc_shape, F32)],
        compiler_params=_cparams("parallel", "parallel", "arbitrary"),
        name=name,
    )(a, w, res, mods, mods)


def _rope_chunk(x, cos, sin_signed, seg):
    lane = lax.broadcasted_iota(jnp.int32, x.shape, 1)
    first = (lane % (2 * seg)) < seg
    partner = jnp.where(first, pltpu.roll(x, LANE - seg, 1), pltpu.roll(x, seg, 1))
    return x * cos + partner * sin_signed


def _rms(x, n_valid):
    return lax.rsqrt(jnp.sum(x * x, axis=-1, keepdims=True) * (1.0 / n_valid) + NORM_EPS)


def _kv_kernel(ckv_ref, kr_ref, gk_ref, gv_ref, wukv_ref, g_kva_ref, g_kn_ref, g_kr_ref, g_gk_ref,
               mcos_ref, msin_ref, gcos_ref, gsin_ref,
               kcat_ref, vm_ref, gko_ref, gvo_ref, *, n_mla_heads, n_kv_heads):
    ckv = ckv_ref[...].astype(F32)
    a = (ckv * _rms(ckv, ckv.shape[-1]) * g_kva_ref[...]).astype(BF16)
    kv = jnp.dot(a, wukv_ref[...], preferred_element_type=F32)

    kr = kr_ref[...].astype(F32)
    krn = kr * _rms(kr, MLA_ROPE_DIM) * g_kr_ref[...]
    kr_rot = _rope_chunk(krn, mcos_ref[...], msin_ref[...], MLA_ROPE_DIM // 4).astype(BF16)

    width = MLA_NOPE_DIM + MLA_V_DIM
    for h in range(n_mla_heads):
        kn = kv[:, h * width:h * width + MLA_NOPE_DIM]
        kcat_ref[h, :, 0:MLA_NOPE_DIM] = (kn * _rms(kn, MLA_NOPE_DIM) * g_kn_ref[...]).astype(BF16)
        kcat_ref[h, :, MLA_NOPE_DIM:2 * MLA_NOPE_DIM] = kr_rot
        vm_ref[h] = kv[:, h * width + MLA_NOPE_DIM:(h + 1) * width].astype(BF16)

    for g in range(n_kv_heads):
        x = gk_ref[:, g * GQA_HEAD_DIM:(g + 1) * GQA_HEAD_DIM].astype(F32)
        xn = x * _rms(x, GQA_HEAD_DIM) * g_gk_ref[...]
        gko_ref[g] = _rope_chunk(xn, gcos_ref[...], gsin_ref[...], GQA_HEAD_DIM // 4).astype(BF16)
        gvo_ref[g] = gv_ref[:, g * GQA_HEAD_DIM:(g + 1) * GQA_HEAD_DIM]


def _kv_side(z, lay, wukv, g_kva, g_kn, g_kr_pad, g_gk, tabs, n_mla_heads, n_kv_heads):
    rows = z.shape[0]
    tm = ROW_TILE
    rkv = wukv.shape[0]
    gw = n_kv_heads * GQA_HEAD_DIM
    row = lambda width: pl.BlockSpec((1, width), lambda i: (0, 0))
    tab = pl.BlockSpec((tm, LANE), lambda i: (i, 0))
    dcat = 2 * MLA_NOPE_DIM
    return pl.pallas_call(
        functools.partial(_kv_kernel, n_mla_heads=n_mla_heads, n_kv_heads=n_kv_heads),
        grid=(rows // tm,),
        in_specs=[
            pl.BlockSpec((tm, rkv), lambda i: (i, lay["c_kv"] // rkv)),
            pl.BlockSpec((tm, LANE), lambda i: (i, lay["k_rope"] // LANE)),
            pl.BlockSpec((tm, gw), lambda i: (i, lay["gk"] // gw)),
            pl.BlockSpec((tm, gw), lambda i: (i, lay["gv"] // gw)),
            pl.BlockSpec(wukv.shape, lambda i: (0, 0)),
            row(rkv), row(LANE), row(LANE), row(LANE),
            tab, tab, tab, tab,
        ],
        out_specs=[
            pl.BlockSpec((n_mla_heads, tm, dcat), lambda i: (0, i, 0)),
            pl.BlockSpec((n_mla_heads, tm, MLA_V_DIM), lambda i: (0, i, 0)),
            pl.BlockSpec((n_kv_heads, tm, GQA_HEAD_DIM), lambda i: (0, i, 0)),
            pl.BlockSpec((n_kv_heads, tm, GQA_HEAD_DIM), lambda i: (0, i, 0)),
        ],
        out_shape=[
            jax.ShapeDtypeStruct((n_mla_heads, rows, dcat), BF16),
            jax.ShapeDtypeStruct((n_mla_heads, rows, MLA_V_DIM), BF16),
            jax.ShapeDtypeStruct((n_kv_heads, rows, GQA_HEAD_DIM), BF16),
            jax.ShapeDtypeStruct((n_kv_heads, rows, GQA_HEAD_DIM), BF16),
        ],
        compiler_params=_cparams("parallel"),
        name="kv_side",
    )(z, z, z, z, wukv, g_kva, g_kn, g_kr_pad, g_gk, *tabs)


def _q_mla_kernel(cq_ref, wuq_ref, g_qa_ref, g_qn_ref, g_qr_ref, mcos_ref, msin_ref, q_ref,
                  *, n_heads, scale):
    cq = cq_ref[...].astype(F32)
    a = (cq * _rms(cq, cq.shape[-1]) * g_qa_ref[...]).astype(BF16)
    qa = jnp.dot(a, wuq_ref[...], preferred_element_type=F32)
    width = 2 * MLA_NOPE_DIM
    for h in range(n_heads):
        qn = qa[:, h * width:h * width + MLA_NOPE_DIM]
        q_ref[h, 0, :, 0:MLA_NOPE_DIM] = (
            qn * (_rms(qn, MLA_NOPE_DIM) * scale) * g_qn_ref[...]).astype(BF16)
        qr = qa[:, h * width + MLA_NOPE_DIM:(h + 1) * width]
        qrn = qr * (_rms(qr, MLA_ROPE_DIM) * scale) * g_qr_ref[...]
        q_ref[h, 0, :, MLA_NOPE_DIM:width] = _rope_chunk(
            qrn, mcos_ref[...], msin_ref[...], MLA_ROPE_DIM // 4).astype(BF16)


def _q_mla(z, lay, wuq, g_qa, g_qn, g_qr_pad, mcos, msin, n_heads, rows):
    tm = ROW_TILE
    rq = wuq.shape[0]
    width = 2 * MLA_NOPE_DIM
    scale = float(MLA_NOPE_DIM + MLA_ROPE_DIM) ** -0.5
    row = lambda w: pl.BlockSpec((1, w), lambda i: (0, 0))
    tab = pl.BlockSpec((tm, LANE), lambda i: (i, 0))
    return pl.pallas_call(
        functools.partial(_q_mla_kernel, n_heads=n_heads, scale=scale),
        grid=(rows // tm,),
        in_specs=[
            pl.BlockSpec((tm, rq), lambda i: (i, lay["c_q"] // rq)),
            pl.BlockSpec(wuq.shape, lambda i: (0, 0)),
            row(rq), row(LANE), row(LANE), tab, tab,
        ],
        out_specs=pl.BlockSpec((n_heads, 1, tm, width), lambda i: (0, 0, i, 0)),
        out_shape=jax.ShapeDtypeStruct((n_heads, 1, z.shape[0], width), BF16),
        compiler_params=_cparams("parallel"),
        name="q_mla",
    )(z, wuq, g_qa, g_qn, g_qr_pad, mcos, msin)


def _q_gqa_kernel(gq_ref, g_ref, gcos_ref, gsin_ref, q_ref, *, rep, scale):
    for r in range(rep):
        x = gq_ref[:, r * GQA_HEAD_DIM:(r + 1) * GQA_HEAD_DIM].astype(F32)
        xn = x * (_rms(x, GQA_HEAD_DIM) * scale) * g_ref[...]
        q_ref[0, r] = _rope_chunk(xn, gcos_ref[...], gsin_ref[...], GQA_HEAD_DIM // 4).astype(BF16)


def _q_gqa(z, lay, g_gq, gcos, gsin, n_kv_heads, rep, rows):
    tm = ROW_TILE
    gw = rep * GQA_HEAD_DIM
    scale = float(GQA_HEAD_DIM) ** -0.5
    return pl.pallas_call(
        functools.partial(_q_gqa_kernel, rep=rep, scale=scale),
        grid=(rows // tm, n_kv_heads),
        in_specs=[
            pl.BlockSpec((tm, gw), lambda i, g: (i, lay["gq"] // gw + g)),
            pl.BlockSpec((1, LANE), lambda i, g: (0, 0)),
            pl.BlockSpec((tm, LANE), lambda i, g: (i, 0)),
            pl.BlockSpec((tm, LANE), lambda i, g: (i, 0)),
        ],
        out_specs=pl.BlockSpec((1, rep, tm, GQA_HEAD_DIM), lambda i, g: (g, 0, i, 0)),
        out_shape=jax.ShapeDtypeStruct((n_kv_heads, rep, z.shape[0], GQA_HEAD_DIM), BF16),
        compiler_params=_cparams("parallel", "parallel"),
        name="q_gqa",
    )(z, g_gq, gcos, gsin)


def _attn_kernel(*refs, rep, tq, n_keys, chunk, aliased):
    if aliased:
        q_ref, k_ref, v_ref, _, o_ref, m_ref, l_ref, acc_ref = refs
    else:
        q_ref, k_ref, v_ref, o_ref, m_ref, l_ref, acc_ref = refs
    dqk = q_ref.shape[-1]
    dv = v_ref.shape[-1]
    q = q_ref[0].reshape(rep * tq, dqk)

    m_ref[...] = jnp.full(m_ref.shape, -jnp.inf, F32)
    l_ref[...] = jnp.zeros(l_ref.shape, F32)
    acc_ref[...] = jnp.zeros(acc_ref.shape, F32)

    def step(k, v):
        s = lax.dot_general(q, k, (((1,), (1,)), ((), ())), preferred_element_type=F32)
        m_prev = m_ref[...]
        m_new = jnp.maximum(m_prev, jnp.max(s, axis=-1, keepdims=True))
        alpha = jnp.exp(m_prev - m_new)
        p = jnp.exp(s - m_new)
        l_ref[...] = alpha * l_ref[...] + jnp.sum(p, axis=-1, keepdims=True)
        acc_ref[...] = alpha * acc_ref[...] + jnp.dot(p.astype(BF16), v, preferred_element_type=F32)
        m_ref[...] = m_new

    n_full = n_keys // chunk
    tail = n_keys - n_full * chunk

    if n_full > 0:
        def body(c, carry):
            start = pl.multiple_of(c * chunk, chunk)
            step(k_ref[0, pl.ds(start, chunk), :], v_ref[0, pl.ds(start, chunk), :])
            return carry
        lax.fori_loop(0, n_full, body, 0)
    if tail > 0:
        step(k_ref[0, n_full * chunk:n_keys, :], v_ref[0, n_full * chunk:n_keys, :])

    out = acc_ref[...] / l_ref[...]
    for r in range(rep):
        o_ref[:, r * dv:(r + 1) * dv] = out[r * tq:(r + 1) * tq].astype(o_ref.dtype)


def _attention(q, k, v, n_q_rows, q_row0, n_keys, key_row0, tq, y_prev=None):
    n_groups, rep, rows, dqk = q.shape
    dv = v.shape[-1]
    qb0 = q_row0 // tq
    kb0 = key_row0 // n_keys
    in_specs = [
        pl.BlockSpec((1, rep, tq, dqk), lambda g, i: (g, 0, qb0 + i, 0)),
        pl.BlockSpec((1, n_keys, dqk), lambda g, i: (g, kb0, 0)),
        pl.BlockSpec((1, n_keys, dv), lambda g, i: (g, kb0, 0)),
    ]
    args = [q, k, v]
    aliases = {}
    if y_prev is not None:
        in_specs.append(pl.BlockSpec(memory_space=pl.ANY))
        args.append(y_prev)
        aliases = {3: 0}
    m_rows = rep * tq
    return pl.pallas_call(
        functools.partial(_attn_kernel, rep=rep, tq=tq, n_keys=n_keys,
                          chunk=min(ATTN_KV_CHUNK, n_keys), aliased=y_prev is not None),
        grid=(n_groups, n_q_rows // tq),
        in_specs=in_specs,
        out_specs=pl.BlockSpec((tq, rep * dv), lambda g, i: (qb0 + i, g)),
        out_shape=jax.ShapeDtypeStruct((rows, n_groups * rep * dv), BF16),
        scratch_shapes=[pltpu.VMEM((m_rows, 1), F32), pltpu.VMEM((m_rows, 1), F32),
                        pltpu.VMEM((m_rows, dv), F32)],
        input_output_aliases=aliases,
        compiler_params=_cparams("parallel", "parallel"),
        name="attention",
    )(*args)


def _conv_kernel(u_ref, gb_ref, gc_ref, up_ref, gcp_ref, un_ref, gcn_ref, w_ref, b_ref, o_ref,
                 scr_ref, *, tm, seq_starts, seq_ends):
    i = pl.program_id(0)
    row0 = i * tm
    p = gc_ref[...].astype(F32) * u_ref[...].astype(F32)
    prev = gcp_ref[...].astype(F32) * up_ref[...].astype(F32)
    nxt = gcn_ref[...].astype(F32) * un_ref[...].astype(F32)
    at_start = functools.reduce(jnp.logical_or, [row0 == s for s in seq_starts])
    at_end = functools.reduce(jnp.logical_or, [row0 + tm == e for e in seq_ends])
    halo = BF16_SUBLANES
    scr_ref[0:halo] = jnp.where(at_start, 0.0, prev)
    scr_ref[halo:halo + tm] = p
    scr_ref[halo + tm:2 * halo + tm] = jnp.where(at_end, 0.0, nxt)
    w = w_ref[...]
    conv = (scr_ref[halo - 1:halo - 1 + tm] * w[0:1] + p * w[1:2]
            + scr_ref[halo + 1:halo + 1 + tm] * w[2:3] + b_ref[...])
    o_ref[...] = (gb_ref[...].astype(F32) * conv).astype(o_ref.dtype)


def _short_conv(z, lay, conv_w, conv_b, rows, seq_starts, seq_ends):
    ch = conv_w.shape[1]
    tm = ROW_TILE
    tc = _pick(ch, (512, 256, 128))
    halo = BF16_SUBLANES
    n_halo_blocks = z.shape[0] // halo
    ou, ob, oc = lay["u_in"] // tc, lay["gate_b"] // tc, lay["gate_c"] // tc
    main = lambda off: pl.BlockSpec((tm, tc), lambda i, j: (i, off + j))
    prev = lambda off: pl.BlockSpec(
        (halo, tc), lambda i, j: (jnp.maximum(i * (tm // halo) - 1, 0), off + j))
    nxt = lambda off: pl.BlockSpec(
        (halo, tc), lambda i, j: (jnp.minimum((i + 1) * (tm // halo), n_halo_blocks - 1), off + j))
    return pl.pallas_call(
        functools.partial(_conv_kernel, tm=tm, seq_starts=seq_starts, seq_ends=seq_ends),
        grid=(rows // tm, ch // tc),
        in_specs=[main(ou), main(ob), main(oc), prev(ou), prev(oc), nxt(ou), nxt(oc),
                  pl.BlockSpec((CONV_K, tc), lambda i, j: (0, j)),
                  pl.BlockSpec((1, tc), lambda i, j: (0, j))],
        out_specs=pl.BlockSpec((tm, tc), lambda i, j: (i, j)),
        out_shape=jax.ShapeDtypeStruct((z.shape[0], ch), BF16),
        scratch_shapes=[pltpu.VMEM((tm + 2 * halo, tc), F32)],
        compiler_params=_cparams("parallel", "parallel"),
        name="short_conv",
    )(z, z, z, z, z, z, z, conv_w, conv_b.reshape(1, ch))


def _branch_kernel(ym_ref, yg_ref, yc_ref, w1_ref, w2_ref, w3_ref, g1_ref, g2_ref, g3_ref, o_ref):
    def term(y_ref, w_ref, g_ref):
        return jax.nn.sigmoid(g_ref[...].astype(F32)) * jnp.dot(
            y_ref[...], w_ref[...], preferred_element_type=F32)
    o_ref[...] = (term(ym_ref, w1_ref, g1_ref) + term(yg_ref, w2_ref, g2_ref)
                  + term(yc_ref, w3_ref, g3_ref)).astype(o_ref.dtype)


def _branch_merge(ym, yg, yc, w1, w2, w3, z, lay, rows, tm):
    bw = ym.shape[1]
    d = w1.shape[1]
    tn = _pick(d, (512, 256, 128))
    g0 = lay["gates"] // tn
    nb = d // tn
    yspec = pl.BlockSpec((tm, bw), lambda i, j: (i, 0))
    wspec = pl.BlockSpec((bw, tn), lambda i, j: (0, j))
    gspec = lambda b: pl.BlockSpec((tm, tn), lambda i, j: (i, g0 + b * nb + j))
    return pl.pallas_call(
        _branch_kernel,
        grid=(rows // tm, nb),
        in_specs=[yspec, yspec, yspec, wspec, wspec, wspec, gspec(0), gspec(1), gspec(2)],
        out_specs=pl.BlockSpec((tm, tn), lambda i, j: (i, j)),
        out_shape=jax.ShapeDtypeStruct((ym.shape[0], d), BF16),
        compiler_params=_cparams("parallel", "parallel"),
        name="branch_merge",
    )(ym, yg, yc, w1, w2, w3, z, z, z)


def _ffn_up_kernel(*refs, with_comb):
    if with_comb:
        x_ref, wg_ref, wu_ref, comb_ref, o_ref = refs
    else:
        x_ref, wg_ref, wu_ref, o_ref = refs
    x = x_ref[...]
    g = jnp.dot(x, wg_ref[...], preferred_element_type=F32)
    u = jnp.dot(x, wu_ref[...], preferred_element_type=F32)
    act = (g * jax.nn.sigmoid(g)) * u
    if with_comb:
        c = comb_ref[0]
        reps = act.shape[1] // LANE
        act = act * (c if reps == 1 else jnp.concatenate([c] * reps, axis=1))
    o_ref[...] = act.astype(o_ref.dtype)


def _ffn_up(x, wg, wu, rows, tm, tn, comb=None, blocks_per_expert=None):
    d = x.shape[1]
    f = wg.shape[1]
    in_specs = [
        pl.BlockSpec((tm, d), lambda i, j: (i, 0)),
        pl.BlockSpec((d, tn), lambda i, j: (0, j)),
        pl.BlockSpec((d, tn), lambda i, j: (0, j)),
    ]
    args = [x, wg, wu]
    if comb is not None:
        in_specs.append(pl.BlockSpec((1, tm, LANE), lambda i, j: (j // blocks_per_expert, i, 0)))
        args.append(comb)
    return pl.pallas_call(
        functools.partial(_ffn_up_kernel, with_comb=comb is not None),
        grid=(rows // tm, f // tn),
        in_specs=in_specs,
        out_specs=pl.BlockSpec((tm, tn), lambda i, j: (i, j)),
        out_shape=jax.ShapeDtypeStruct((x.shape[0], f), BF16),
        compiler_params=_cparams("parallel", "parallel"),
        name="ffn_up",
    )(*args)


def _router_kernel(x_ref, r_ref, b_ref, comb_ref, *, n_experts):
    raw = jnp.dot(x_ref[...], r_ref[...], preferred_element_type=F32)
    logits = raw + pltpu.roll(raw, LANE - n_experts, 1) + b_ref[...]
    lane = lax.broadcasted_iota(jnp.int32, logits.shape, 1)
    neg = -jnp.inf
    lg = jnp.where(lane < n_experts, logits, neg)
    v0 = jnp.max(lg, axis=-1, keepdims=True)
    i0 = jnp.min(jnp.where(lg == v0, lane, LANE), axis=-1, keepdims=True)
    rest = jnp.where(lane == i0, neg, lg)
    v1 = jnp.max(rest, axis=-1, keepdims=True)
    i1 = jnp.min(jnp.where(rest == v1, lane, LANE), axis=-1, keepdims=True)
    e = jnp.exp(v1 - v0)
    w0 = 1.0 / (1.0 + e)
    w1 = e * w0
    for ex in range(n_experts):
        col = jnp.where(i0 == ex, w0, 0.0) + jnp.where(i1 == ex, w1, 0.0)
        comb_ref[ex] = jnp.broadcast_to(col, comb_ref.shape[1:])


def _router(x, r_pad, b_pad, n_experts, rows):
    d = x.shape[1]
    tm = ROW_TILE
    return pl.pallas_call(
        functools.partial(_router_kernel, n_experts=n_experts),
        grid=(rows // tm,),
        in_specs=[
            pl.BlockSpec((tm, d), lambda i: (i, 0)),
            pl.BlockSpec((d, LANE), lambda i: (0, 0)),
            pl.BlockSpec((1, LANE), lambda i: (0, 0)),
        ],
        out_specs=pl.BlockSpec((n_experts, tm, LANE), lambda i: (0, i, 0)),
        out_shape=jax.ShapeDtypeStruct((n_experts, rows, LANE), F32),
        compiler_params=_cparams("parallel"),
        name="router",
    )(x, r_pad, b_pad)


def _z_layout(rq, rkv, n_kv_heads, n_gqa_heads, conv_ch, d):
    gw = n_kv_heads * GQA_HEAD_DIM
    qgw = (n_gqa_heads // n_kv_heads) * GQA_HEAD_DIM
    conv_blk = _pick(conv_ch, (512, 256, 128))
    gate_blk = _pick(d, (512, 256, 128))
    segs = [("c_q", rq, rq), ("k_rope", LANE, LANE), ("c_kv", rkv, rkv), ("gk", gw, gw),
            ("gv", gw, gw), ("gq", n_gqa_heads * GQA_HEAD_DIM, qgw),
            ("u_in", conv_ch, conv_blk), ("gate_b", conv_ch, conv_blk), ("gate_c", conv_ch, conv_blk),
            ("gates", N_BRANCH * d, gate_blk)]
    lay, off = {}, 0
    for name, width, align in segs:
        off = _round_up(off, align)
        lay[name] = off
        off += width
    lay["total"] = _round_up(off, 2 * LANE)
    return lay


def _relayout_w_in(w, lay, rq, rkv, n_kv_heads, n_gqa_heads, conv_ch, d):
    gw = n_kv_heads * GQA_HEAD_DIM
    widths = [("c_kv", rkv), ("k_rope", MLA_ROPE_DIM), ("gk", gw), ("gv", gw), ("c_q", rq),
              ("gq", n_gqa_heads * GQA_HEAD_DIM), ("u_in", conv_ch), ("gate_b", conv_ch),
              ("gate_c", conv_ch), ("gates", N_BRANCH * d)]
    out = jnp.zeros((w.shape[0], lay["total"]), BF16)
    src = 0
    for name, width in widths:
        out = lax.dynamic_update_slice(out, w[:, src:src + width].astype(BF16), (0, lay[name]))
        src += width
    return out


def _rope_tables(n_lat, n_ctx, dim):
    rows = n_lat // GRID_W
    row_ids = jnp.repeat(jnp.arange(rows), GRID_W).astype(F32)
    col_ids = (jnp.arange(rows * GRID_W) % GRID_W).astype(F32)
    half = dim // 2
    inv = ROPE_THETA ** (-jnp.arange(0, half, 2, dtype=F32) / half)
    ang_r = row_ids[:, None] * inv[None, :]
    ang_c = col_ids[:, None] * inv[None, :]
    cr, sr, cc, sc = jnp.cos(ang_r), jnp.sin(ang_r), jnp.cos(ang_c), jnp.sin(ang_c)
    pad = jnp.zeros((n_lat, LANE - dim), F32)
    cos = jnp.concatenate([cr, cr, cc, cc, pad], axis=1)
    sin = jnp.concatenate([-sr, sr, -sc, sc, pad], axis=1)
    lane = jnp.arange(LANE)[None, :]
    cos_ctx = jnp.broadcast_to(jnp.where(lane < dim, 1.0, 0.0).astype(F32), (n_ctx, LANE))
    return (jnp.concatenate([cos, cos_ctx], axis=0),
            jnp.concatenate([sin, jnp.zeros((n_ctx, LANE), F32)], axis=0))


def _pad_cols(a, width):
    return jnp.pad(a, ((0, 0), (0, width - a.shape[1])))


def kernel(x, c, ctx, c_ctx, ada_w, ada_b, norm_mix, norm_ffn, w_in, mla_q_a_norm, mla_kv_a_norm, mla_w_uq, mla_w_ukv, mla_qn_norm, mla_qr_norm, mla_kn_norm, mla_kr_norm, gqa_q_norm, gqa_k_norm, conv_w, conv_b, w_br_mla, w_br_gqa, w_br_conv, w_out, ffn_w_gate, ffn_w_up, ffn_w_down, moe_router, moe_router_b, moe_w_gate, moe_w_up, moe_w_down):
    batch, n_lat, d = x.shape
    n_ctx = ctx.shape[1]
    assert batch == 1, "kernels are written for one sample"
    depth = w_in.shape[0]
    rq, rkv = mla_w_uq.shape[1], mla_w_ukv.shape[1]
    n_mla_heads = mla_w_uq.shape[2] // (MLA_NOPE_DIM + MLA_ROPE_DIM)
    conv_ch = conv_w.shape[2]
    n_gqa_heads = w_br_gqa.shape[1] // GQA_HEAD_DIM
    kv_cols = w_in.shape[2] - (rq + n_gqa_heads * GQA_HEAD_DIM + 3 * conv_ch + N_BRANCH * d)
    n_kv_heads = (kv_cols - rkv - MLA_ROPE_DIM) // (2 * GQA_HEAD_DIM)
    rep = n_gqa_heads // n_kv_heads
    n_all = n_lat + n_ctx
    assert n_lat % ROW_TILE == 0 and n_ctx % ROW_TILE == 0

    lay = _z_layout(rq, rkv, n_kv_heads, n_gqa_heads, conv_ch, d)
    mla_tabs = _rope_tables(n_lat, n_ctx, MLA_ROPE_DIM)
    gqa_tabs = _rope_tables(n_lat, n_ctx, GQA_HEAD_DIM)

    cvecs = jnp.zeros((8, d), F32).at[0].set(c[0]).at[1].set(c_ctx)
    mods_all = _ada_mod(cvecs, ada_w, ada_b)

    h = jnp.concatenate([x[0], ctx[0]], axis=0)

    tm_all = _pick(n_all, (640, 512, 256))
    tm_lat = _pick(n_lat, (1024, 512, 256))
    tq_mla = _pick(n_lat, (1024, 512, 256))
    tq_gqa = _pick(n_lat, (256,))

    for i in range(depth):
        last = i == depth - 1
        rows = n_lat if last else n_all
        tm = tm_lat if last else tm_all
        mods = mods_all[i, 0:2].reshape(2 * 6, 1, d)

        xn = _norm_mod(h, norm_mix[i], mods, 0, n_lat, n_all)
        w_in_i = _relayout_w_in(w_in[i], lay, rq, rkv, n_kv_heads, n_gqa_heads, conv_ch, d)
        tn_in = _pick(lay["total"], (1536, 1280, 1024, 768, 512, 256))
        z = _matmul(xn, w_in_i, n_all, tm_all, tn_in, BF16, "w_in")

        wukv = mla_w_ukv[i].astype(BF16)
        kcat, vm, gk, gv = _kv_side(
            z, lay, wukv, mla_kv_a_norm[i][None], mla_kn_norm[i][None],
            _pad_cols(mla_kr_norm[i][None], LANE), gqa_k_norm[i][None],
            mla_tabs + gqa_tabs, n_mla_heads, n_kv_heads)

        wuq = mla_w_uq[i].reshape(rq, n_mla_heads, MLA_NOPE_DIM + MLA_ROPE_DIM)
        wuq = jnp.pad(wuq, ((0, 0), (0, 0), (0, 2 * MLA_NOPE_DIM - wuq.shape[2])))
        wuq = wuq.reshape(rq, n_mla_heads * 2 * MLA_NOPE_DIM).astype(BF16)
        qm = _q_mla(z, lay, wuq, mla_q_a_norm[i][None], mla_qn_norm[i][None],
                    _pad_cols(mla_qr_norm[i][None], LANE), *mla_tabs, n_mla_heads, rows)
        qg = _q_gqa(z, lay, gqa_q_norm[i][None], *gqa_tabs, n_kv_heads, rep, rows)

        y_mla = _attention(qm, kcat, vm, n_lat, 0, n_all, 0, tq_mla)
        y_gqa = _attention(qg, gk, gv, n_lat, 0, n_all, 0, tq_gqa)
        if not last:
            y_mla = _attention(qm, kcat, vm, n_ctx, n_lat, n_ctx, n_lat, ROW_TILE, y_prev=y_mla)
            y_gqa = _attention(qg, gk, gv, n_ctx, n_lat, n_ctx, n_lat, ROW_TILE, y_prev=y_gqa)

        y_conv = _short_conv(z, lay, conv_w[i], conv_b[i], rows,
                             seq_starts=(0, n_lat), seq_ends=(n_lat, n_all))
        merged = _branch_merge(y_mla, y_gqa, y_conv, w_br_mla[i].astype(BF16),
                               w_br_gqa[i].astype(BF16), w_br_conv[i].astype(BF16), z, lay, rows, tm)
        tn_d = _pick(d, (1024, 512, 256))
        h = _matmul_residual(merged, w_out[i].astype(BF16), h, mods, 2, n_lat, rows, tm,
                             _pick(d, (512, 256)), d, "w_out")

        xf = _norm_mod(h, norm_ffn[i], mods, 3, n_lat, rows)
        j = i // 2
        if i % 2 == 0:
            f = ffn_w_gate.shape[2]
            f_pad = _round_up(f, 1024)
            wg = _pad_cols(ffn_w_gate[j], f_pad).astype(BF16)
            wu = _pad_cols(ffn_w_up[j], f_pad).astype(BF16)
            wd = jnp.pad(ffn_w_down[j], ((0, f_pad - f), (0, 0))).astype(BF16)
            act = _ffn_up(xf, wg, wu, rows, tm, 512)
            tk = _pick(f_pad, (2816, 2048, 1024))
        else:
            n_experts, _, fe = moe_w_gate.shape[1:]
            fe_pad = _round_up(fe, 2 * LANE)
            r = moe_router[j]
            r_hi = r.astype(BF16)
            r_lo = (r - r_hi.astype(F32)).astype(BF16)
            r_pad = _pad_cols(jnp.concatenate([r_hi, r_lo], axis=1), LANE)
            comb = _router(xf, r_pad, _pad_cols(moe_router_b[j][None], LANE), n_experts, rows)
            pad_e = lambda w: jnp.pad(w, ((0, 0), (0, 0), (0, fe_pad - fe)))
            wg = jnp.moveaxis(pad_e(moe_w_gate[j]), 0, 1).reshape(d, n_experts * fe_pad).astype(BF16)
            wu = jnp.moveaxis(pad_e(moe_w_up[j]), 0, 1).reshape(d, n_experts * fe_pad).astype(BF16)
            wd = jnp.pad(moe_w_down[j], ((0, 0), (0, fe_pad - fe), (0, 0)))
            wd = wd.reshape(n_experts * fe_pad, d).astype(BF16)
            act = _ffn_up(xf, wg, wu, rows, tm, 2 * LANE, comb=comb,
                          blocks_per_expert=fe_pad // (2 * LANE))
            tk = fe_pad
        h = _matmul_residual(act, wd, h, mods, 5, n_lat, rows, tm, tn_d, tk, "ffn_down")

    return h[:n_lat][None]
```
